```python
import jax
import jax.numpy as jnp
from jax import lax
import numpy as np

D_MODEL = 1024
BATCH = 2
SEQ = 8192
DEPTH = 4

CTX_LEN = 256
GRID_W = 64

M_HEADS = 8
M_HEAD_DIM = 128
M_WIDTH = M_HEADS * M_HEAD_DIM
M_CHUNK = 128
M_CONV = 3
M_STAB_INIT = -1e30

S_GROUPS = 8
S_WIDTH = 1024
S_GROUP_DIM = S_WIDTH // S_GROUPS
S_CHUNK = 128

N_KEYS = 128
N_EXPERTS = N_KEYS * N_KEYS
PEER_HEADS = 8
PEER_TOPK = 16
PEER_QDIM = 256
PEER_HALF = PEER_QDIM // 2
PEER_BLOCK = 128

N_GATES = 4 * M_HEADS
IN_SPLITS = (2 * M_WIDTH, 3 * M_WIDTH, 4 * M_WIDTH, 4 * M_WIDTH + N_GATES,
             4 * M_WIDTH + N_GATES + S_WIDTH, 4 * M_WIDTH + N_GATES + 2 * S_WIDTH,
             4 * M_WIDTH + N_GATES + 2 * S_WIDTH + D_MODEL)
IN_WIDTH = IN_SPLITS[-1] + D_MODEL

DN_ALPHA = (2 * DEPTH) ** 0.25
DN_BETA = (8 * DEPTH) ** -0.25
LN_EPS = 1e-5

kernel_name = 'hybrid_mlstm_sgu_peer_dit'


def _standardize(a):
    af = a.astype(jnp.float32)
    mu = jnp.mean(af, axis=-1, keepdims=True)
    var = jnp.mean(jnp.square(af - mu), axis=-1, keepdims=True)
    return (af - mu) * lax.rsqrt(var + LN_EPS)


def layer_norm(a, w, b):
    return (_standardize(a) * w + b).astype(a.dtype)


def conv_grid(a, w, b, rows):
    B_, T, C = a.shape
    img = a.reshape(B_, rows, GRID_W, C)
    out = lax.conv_general_dilated(img, w[:, :, None, :], window_strides=(1, 1), padding='SAME',
                                   dimension_numbers=('NHWC', 'HWIO', 'NHWC'),
                                   feature_group_count=C)
    return out.reshape(B_, T, C) + b


def conv_seq(a, w_row, b):
    ap = jnp.pad(a, ((0, 0), (1, 1), (0, 0)))
    return ap[:, :-2] * w_row[0] + ap[:, 1:-1] * w_row[1] + ap[:, 2:] * w_row[2] + b


def zero_state(b):
    return (jnp.zeros((b, M_HEADS, M_HEAD_DIM, M_HEAD_DIM), jnp.float32),
            jnp.zeros((b, M_HEADS, M_HEAD_DIM), jnp.float32),
            jnp.full((b, M_HEADS), M_STAB_INIT, jnp.float32))


def mlstm_chunked(q, k, v, log_i, log_f, state):
    B_, H, T, dh = q.shape
    nc = T // M_CHUNK

    def to_chunks(a):
        a = a.reshape(B_, H, nc, M_CHUNK, *a.shape[3:])
        return jnp.moveaxis(a, 2, 0)

    xs = (to_chunks(q), to_chunks(k), to_chunks(v), to_chunks(log_i), to_chunks(log_f))
    tri = jnp.tril(jnp.ones((M_CHUNK, M_CHUNK), dtype=bool))

    def step(carry, inp):
        C, n, m = carry
        qc, kc, vc, ic, fc = inp
        b = jnp.cumsum(fc, axis=-1)
        d = jnp.where(tri, b[..., :, None] - b[..., None, :] + ic[..., None, :], -jnp.inf)
        inter = b + m[..., None]
        m_t = jnp.maximum(inter, jnp.max(d, axis=-1))
        a_t = jnp.exp(inter - m_t)
        s = jnp.einsum('bhtk,bhsk->bhts', qc, kc) * jnp.exp(d - m_t[..., None])
        num = a_t[..., None] * jnp.einsum('bhvk,bhtk->bhtv', C, qc) + jnp.einsum('bhts,bhsv->bhtv', s, vc)
        den = a_t * jnp.einsum('bhk,bhtk->bht', n, qc) + jnp.sum(s, axis=-1)
        h = num / jnp.maximum(jnp.abs(den), jnp.exp(-m_t))[..., None]
        g = b[..., -1]
        dl = g[..., None] - b + ic
        m_new = jnp.maximum(g + m, jnp.max(dl, axis=-1))
        wl = jnp.exp(dl - m_new[..., None])
        a_l = jnp.exp(g + m - m_new)
        C = a_l[..., None, None] * C + jnp.einsum('bhsv,bhsk->bhvk', vc * wl[..., None], kc)
        n = a_l[..., None] * n + jnp.einsum('bhs,bhsk->bhk', wl, kc)
        return (C, n, m_new), h

    final, hs = lax.scan(step, state, xs)
    return jnp.moveaxis(hs, 0, 2).reshape(B_, H, T, dh), final


def mlstm_prepare(qk, v, gates, gate_b):
    B_, T, _ = v.shape

    def heads(a):
        return a.reshape(B_, T, M_HEADS, M_HEAD_DIM).transpose(0, 2, 1, 3).astype(jnp.float32)

    q, k = jnp.split(qk, 2, axis=-1)
    g = (gates + gate_b).astype(jnp.float32).transpose(0, 2, 1)
    i_f, f_f, i_b, f_b = jnp.split(g, 4, axis=1)
    return (heads(q) * M_HEAD_DIM ** -0.5, heads(k), heads(v),
            (i_f, jax.nn.log_sigmoid(f_f)), (i_b, jax.nn.log_sigmoid(f_b)))


def mlstm_bidir(q, k, v, fwd, bwd, state_f, state_b):
    h_f, st_f = mlstm_chunked(q, k, v, fwd[0], fwd[1], state_f)
    rev = lambda a: jnp.flip(a, axis=2)
    h_b, st_b = mlstm_chunked(rev(q), rev(k), rev(v), rev(bwd[0]), rev(bwd[1]), state_b)
    return h_f + rev(h_b), st_f, st_b


def mlstm_readout(h, o, norm_w):
    B_, H, T, dh = h.shape
    hn = _standardize(h).transpose(0, 2, 1, 3).reshape(B_, T, M_WIDTH) * norm_w
    return (jax.nn.sigmoid(o.astype(jnp.float32)) * hn).astype(o.dtype)


def spatial_gating(u, vg, norm_w, w_s, b_s):
    B_, T, _ = u.shape
    zv = (_standardize(jax.nn.gelu(vg)) * norm_w).astype(vg.dtype)
    zv = zv.reshape(B_, T // S_CHUNK, S_CHUNK, S_GROUPS, S_GROUP_DIM)
    mixed = jnp.einsum('gpq,bnqgc->bnpgc', w_s, zv) + b_s.T[:, :, None]
    return jax.nn.gelu(u) * mixed.reshape(B_, T, S_WIDTH)


def branch_merge(h_a, h_b, ga, gb, branch_b, p_a, p_b, w_out):
    ba, bb = jnp.split(branch_b, 2)
    merged = jax.nn.sigmoid(ga + ba) * (h_a @ p_a) + jax.nn.sigmoid(gb + bb) * (h_b @ p_b)
    return merged @ w_out


def token_mixer(hc, hx, rows, w_in, qk_conv_w, qk_conv_b, gate_b, mlstm_norm_w, sgu_norm_w,
                sgu_w, sgu_b, branch_b, p_a, p_b, w_out, ctx_out):
    qk_c, v_c, o_c, gt_c, u_c, vg_c, ga_c, gb_c = jnp.split(hc @ w_in, IN_SPLITS, axis=-1)
    qk_x, v_x, o_x, gt_x, u_x, vg_x, ga_x, gb_x = jnp.split(hx @ w_in, IN_SPLITS, axis=-1)
    qk_c = jax.nn.silu(conv_seq(qk_c, qk_conv_w[M_CONV // 2], qk_conv_b))
    qk_x = jax.nn.silu(conv_grid(qk_x, qk_conv_w, qk_conv_b, rows))
    q_c, k_c, vh_c, fwd_c, bwd_c = mlstm_prepare(qk_c, v_c, gt_c, gate_b)
    q_x, k_x, vh_x, fwd_x, bwd_x = mlstm_prepare(qk_x, v_x, gt_x, gate_b)
    nb = hc.shape[0]
    h_mc, st_f, st_b = mlstm_bidir(q_c, k_c, vh_c, fwd_c, bwd_c, zero_state(nb), zero_state(nb))
    h_mx, _, _ = mlstm_bidir(q_x, k_x, vh_x, fwd_x, bwd_x, st_f, st_b)
    y_x = branch_merge(mlstm_readout(h_mx, o_x, mlstm_norm_w),
                       spatial_gating(u_x, vg_x, sgu_norm_w, sgu_w, sgu_b),
                       ga_x, gb_x, branch_b, p_a, p_b, w_out)
    if not ctx_out:
        return None, y_x
    y_c = branch_merge(mlstm_readout(h_mc, o_c, mlstm_norm_w),
                       spatial_gating(u_c, vg_c, sgu_norm_w, sgu_w, sgu_b),
                       ga_c, gb_c, branch_b, p_a, p_b, w_out)
    return y_c, y_x


def peer(h, w_q, sub_keys, expert_u, expert_v):
    B_, T, D = h.shape
    blocks = h.reshape(B_ * T // PEER_BLOCK, PEER_BLOCK, D)

    def retrieve(xb):
        qry = (xb @ w_q).reshape(PEER_BLOCK, PEER_HEADS, 2, PEER_HALF)
        scores = jnp.einsum('thpd,hpkd->thpk', qry, sub_keys)
        s_top, i_top = lax.top_k(scores, PEER_TOPK)
        cand_s = (s_top[:, :, 0, :, None] + s_top[:, :, 1, None, :]).reshape(
            PEER_BLOCK, PEER_HEADS, PEER_TOPK * PEER_TOPK)
        cand_i = (i_top[:, :, 0, :, None] * N_KEYS + i_top[:, :, 1, None, :]).reshape(
            PEER_BLOCK, PEER_HEADS, PEER_TOPK * PEER_TOPK)
        s_fin, pos = lax.top_k(cand_s, PEER_TOPK)
        idx = jnp.take_along_axis(cand_i, pos, axis=-1)
        gate = jax.nn.softmax(s_fin.astype(jnp.float32), axis=-1).astype(xb.dtype)
        act = jax.nn.gelu(jnp.einsum('thed,td->the', jnp.take(expert_u, idx, axis=0), xb))
        return jnp.einsum('the,thed->td', gate * act, jnp.take(expert_v, idx, axis=0))

    return lax.map(retrieve, blocks).reshape(B_, T, D)


def setup_inputs(seed: int = 0) -> dict:
    key = jax.random.key(seed)
    ks = jax.random.split(key, 24)

    def nrm(k, shape, s=1.0):
        return s * jax.random.normal(k, shape, jnp.float32)

    f_bias = jnp.linspace(3.0, 6.0, M_HEADS, dtype=jnp.float32)
    zeros_h = jnp.zeros((M_HEADS,), jnp.float32)
    gate_base = jnp.concatenate([zeros_h, f_bias, zeros_h, f_bias])
    return {
        'x': nrm(ks[0], (BATCH, SEQ, D_MODEL)),
        'c': nrm(ks[1], (BATCH, D_MODEL)),
        'ctx': nrm(ks[2], (BATCH, CTX_LEN, D_MODEL)),
        'c_ctx': nrm(ks[3], (D_MODEL,)),
        'w_mod': nrm(ks[4], (DEPTH, D_MODEL, 6 * D_MODEL), 0.5 * D_MODEL ** -0.5),
        'b_mod': nrm(ks[5], (DEPTH, 6 * D_MODEL), 0.02),
        'w_in': nrm(ks[6], (DEPTH, D_MODEL, IN_WIDTH), D_MODEL ** -0.5),
        'qk_conv_w': nrm(ks[7], (DEPTH, M_CONV, M_CONV, 2 * M_WIDTH), 1.0 / M_CONV),
        'qk_conv_b': nrm(ks[8], (DEPTH, 2 * M_WIDTH), 0.02),
        'gate_b': gate_base + nrm(ks[9], (DEPTH, N_GATES), 0.1),
        'mlstm_norm_w': 1.0 + nrm(ks[10], (DEPTH, M_WIDTH), 0.1),
        'sgu_norm_w': 1.0 + nrm(ks[11], (DEPTH, S_WIDTH), 0.1),
        'sgu_w': nrm(ks[12], (DEPTH, S_GROUPS, S_CHUNK, S_CHUNK), S_CHUNK ** -0.5),
        'sgu_b': 1.0 + nrm(ks[13], (DEPTH, S_GROUPS, S_CHUNK), 0.1),
        'branch_b': nrm(ks[14], (DEPTH, 2 * D_MODEL), 0.1),
        'p_a': nrm(ks[15], (DEPTH, M_WIDTH, D_MODEL), DN_BETA * M_WIDTH ** -0.5),
        'p_b': nrm(ks[16], (DEPTH, S_WIDTH, D_MODEL), DN_BETA * S_WIDTH ** -0.5),
        'w_out': nrm(ks[17], (DEPTH, D_MODEL, D_MODEL), DN_BETA * D_MODEL ** -0.5),
        'ln_w': 1.0 + nrm(ks[18], (DEPTH, 2, D_MODEL), 0.1),
        'ln_b': nrm(ks[19], (DEPTH, 2, D_MODEL), 0.02),
        'peer_wq': nrm(ks[20], (DEPTH, D_MODEL, PEER_HEADS * PEER_QDIM), D_MODEL ** -0.5),
        'peer_keys': nrm(ks[21], (DEPTH, PEER_HEADS, 2, N_KEYS, PEER_HALF), PEER_HALF ** -0.5),
        'expert_u': nrm(ks[22], (DEPTH, N_EXPERTS, D_MODEL), D_MODEL ** -0.5),
        'expert_v': nrm(ks[23], (DEPTH, N_EXPERTS, D_MODEL), DN_BETA),
    }


def reference(x, c, ctx, c_ctx, w_mod, b_mod, w_in, qk_conv_w, qk_conv_b, gate_b, mlstm_norm_w,
              sgu_norm_w, sgu_w, sgu_b, branch_b, p_a, p_b, w_out, ln_w, ln_b, peer_wq, peer_keys,
              expert_u, expert_v):
    rows = x.shape[1] // GRID_W
    cond_x = jax.nn.silu(c)
    cond_c = jax.nn.silu(c_ctx)
    zx, zc = x, ctx
    for l in range(DEPTH):
        ctx_live = l < DEPTH - 1
        mx = jnp.split((cond_x @ w_mod[l] + b_mod[l])[:, None, :], 6, axis=-1)
        mc = jnp.split(cond_c @ w_mod[l] + b_mod[l], 6, axis=-1)
        hx = zx * (1.0 + mx[1]) + mx[0]
        hc = zc * (1.0 + mc[1]) + mc[0]
        y_c, y_x = token_mixer(hc, hx, rows, w_in[l], qk_conv_w[l], qk_conv_b[l], gate_b[l],
                               mlstm_norm_w[l], sgu_norm_w[l], sgu_w[l], sgu_b[l], branch_b[l],
                               p_a[l], p_b[l], w_out[l], ctx_live)
        zx = layer_norm(DN_ALPHA * zx + mx[2] * y_x, ln_w[l, 0], ln_b[l, 0])
        hx = zx * (1.0 + mx[4]) + mx[3]
        zx = layer_norm(DN_ALPHA * zx + mx[5] * peer(hx, peer_wq[l], peer_keys[l], expert_u[l], expert_v[l]),
                        ln_w[l, 1], ln_b[l, 1])
        if ctx_live:
            zc = layer_norm(DN_ALPHA * zc + mc[2] * y_c, ln_w[l, 0], ln_b[l, 0])
            hc = zc * (1.0 + mc[4]) + mc[3]
            zc = layer_norm(DN_ALPHA * zc + mc[5] * peer(hc, peer_wq[l], peer_keys[l], expert_u[l], expert_v[l]),
                            ln_w[l, 1], ln_b[l, 1])
    return zx
```

```python
import functools

import jax
import jax.numpy as jnp
from jax import lax
from jax.experimental import pallas as pl
from jax.experimental.pallas import tpu as pltpu

F32 = jnp.float32
BF16 = jnp.bfloat16

GRID_W = 64
M_HEADS = 8
M_HEAD_DIM = 128
M_CHUNK = 128
M_STAB_INIT = -1e30
N_GATES = 4 * M_HEADS
S_GROUPS = 8
S_CHUNK = 128
N_KEYS = 128
PEER_HEADS = 8
PEER_TOPK = 16
PEER_HALF = 128
LN_EPS = 1e-5

LANES = 128
VMEM_LIMIT_BYTES = 56 * 1024 * 1024

_CAND_COUNTS = tuple(PEER_TOPK // (r + 1) for r in range(PEER_TOPK))
_CAND_OFFS = tuple(sum(_CAND_COUNTS[:r]) for r in range(PEER_TOPK))
_N_CAND = sum(_CAND_COUNTS)
_N_CAND_PAD = -(-_N_CAND // 8) * 8
_NOT_RANKED = 99.0


def _cparams(sem):
    return pltpu.CompilerParams(dimension_semantics=sem, vmem_limit_bytes=VMEM_LIMIT_BYTES)


def _bdot(a, b):
    return jnp.dot(a.astype(BF16), b.astype(BF16), preferred_element_type=F32)


def _split3(x):
    hi = x.astype(BF16)
    r1 = x - hi.astype(F32)
    mid = r1.astype(BF16)
    lo = (r1 - mid.astype(F32)).astype(BF16)
    return hi, mid, lo


def _dot01_left(t01, x):
    hi, mid, lo = _split3(x)
    d = lambda p: jnp.dot(t01, p, preferred_element_type=F32)
    return (d(lo) + d(mid)) + d(hi)


def _dot01_right(x, t01):
    hi, mid, lo = _split3(x)
    d = lambda p: jnp.dot(p, t01, preferred_element_type=F32)
    return (d(lo) + d(mid)) + d(hi)


def _log_sigmoid(x):
    return jnp.minimum(x, 0.0) - jnp.log1p(jnp.exp(-jnp.abs(x)))


def _standardize(a):
    mu = jnp.mean(a, axis=-1, keepdims=True)
    c = a - mu
    var = jnp.mean(c * c, axis=-1, keepdims=True)
    return c * lax.rsqrt(var + LN_EPS)


def _seg_of_row(row0, nctx, seq):
    return jnp.where(row0 < nctx, 0, 1 + jnp.maximum(row0 - nctx, 0) // seq)


def _row_tile(nctx, seq, want):
    t = want
    while nctx % t or seq % t:
        t //= 2
    assert t >= M_CHUNK, (nctx, seq)
    return t


def _mod_kernel(c_ref, w_ref, b_ref, o_ref):
    cond = jax.nn.silu(c_ref[...])
    o_ref[0] = _bdot(cond, w_ref[0]) + b_ref[0]


def _mod_table(cvec, w_mod, b_mod):
    depth, d, n = w_mod.shape
    tn = 1024
    return pl.pallas_call(
        _mod_kernel,
        grid=(depth, n // tn),
        in_specs=[
            pl.BlockSpec((8, d), lambda l, j: (0, 0)),
            pl.BlockSpec((1, d, tn), lambda l, j: (l, 0, j)),
            pl.BlockSpec((1, 1, tn), lambda l, j: (l, 0, j)),
        ],
        out_specs=pl.BlockSpec((1, 8, tn), lambda l, j: (l, 0, j)),
        out_shape=jax.ShapeDtypeStruct((depth, 8, n), F32),
        compiler_params=_cparams(("arbitrary", "arbitrary")),
        name="adaln_table",
    )(cvec, w_mod, b_mod.reshape(depth, 1, n))


def _modmm_kernel(z_ref, mod_ref, w_ref, o_ref, *, which):
    m = mod_ref[0]
    shift = m[3 * which:3 * which + 1, :]
    scale = m[3 * which + 1:3 * which + 2, :]
    h = z_ref[...] * (1.0 + scale) + shift
    o_ref[...] = jnp.dot(h.astype(BF16), w_ref[...], preferred_element_type=F32)


def _mod_matmul(z, mod, w_bf, *, which, nctx, seq, tm, tn, name):
    ntok, d = z.shape
    n = w_bf.shape[1]
    seg = lambda j, i: (_seg_of_row(i * tm, nctx, seq), 0, 0)
    return pl.pallas_call(
        functools.partial(_modmm_kernel, which=which),
        grid=(n // tn, ntok // tm),
        in_specs=[
            pl.BlockSpec((tm, d), lambda j, i: (i, 0)),
            pl.BlockSpec((1, 6, d), seg),
            pl.BlockSpec((d, tn), lambda j, i: (0, j)),
        ],
        out_specs=pl.BlockSpec((tm, tn), lambda j, i: (i, j)),
        out_shape=jax.ShapeDtypeStruct((ntok, n), F32),
        compiler_params=_cparams(("arbitrary", "arbitrary")),
        name=name,
    )(z, mod, w_bf)


def _conv_kernel(prev_ref, cur_ref, next_ref, w_ref, b_ref, o_ref, win_ref, *,
                 n_ctx_chunks, ctx_chunks_per_seq, lat_chunks_per_seq, ctx_len, rows):
    g = pl.program_id(0)
    c = M_CHUNK
    win_ref[0:c, :] = prev_ref[...]
    win_ref[c:2 * c, :] = cur_ref[...]
    win_ref[2 * c:3 * c, :] = next_ref[...]
    w = w_ref[...]
    r = lax.broadcasted_iota(jnp.int32, cur_ref.shape, 0)

    def tap(off):
        return win_ref[c + off:2 * c + off, :]

    def finish(acc):
        acc = acc + b_ref[...]
        o_ref[...] = acc * jax.nn.sigmoid(acc)

    @pl.when(g < n_ctx_chunks)
    def _context():
        t = (g % ctx_chunks_per_seq) * c + r
        acc = cur_ref[...] * w[4:5, :]
        acc = acc + jnp.where(t > 0, tap(-1), 0.0) * w[3:4, :]
        acc = acc + jnp.where(t < ctx_len - 1, tap(1), 0.0) * w[5:6, :]
        finish(acc)

    @pl.when(g >= n_ctx_chunks)
    def _latent():
        p = ((g - n_ctx_chunks) % lat_chunks_per_seq) * c + r
        col = jnp.bitwise_and(p, GRID_W - 1)
        row = jnp.right_shift(p, GRID_W.bit_length() - 1)
        acc = jnp.zeros(cur_ref.shape, F32)
        for dy in (-1, 0, 1):
            row_ok = (row + dy >= 0) & (row + dy < rows)
            for dx in (-1, 0, 1):
                ok = row_ok & (col + dx >= 0) & (col + dx < GRID_W)
                k = (dy + 1) * 3 + (dx + 1)
                acc = acc + jnp.where(ok, tap(dy * GRID_W + dx), 0.0) * w[k:k + 1, :]
        finish(acc)


def _qk_conv(proj, conv_w, conv_b, *, batch, ctx_len, seq):
    ntok = proj.shape[0]
    nch = ntok // M_CHUNK
    c2 = conv_w.shape[-1]
    cb = 512
    kern = functools.partial(
        _conv_kernel,
        n_ctx_chunks=batch * ctx_len // M_CHUNK,
        ctx_chunks_per_seq=ctx_len // M_CHUNK,
        lat_chunks_per_seq=seq // M_CHUNK,
        ctx_len=ctx_len,
        rows=seq // GRID_W,
    )
    return pl.pallas_call(
        kern,
        grid=(nch, c2 // cb),
        in_specs=[
            pl.BlockSpec((M_CHUNK, cb), lambda g, j: (jnp.maximum(g - 1, 0), j)),
            pl.BlockSpec((M_CHUNK, cb), lambda g, j: (g, j)),
            pl.BlockSpec((M_CHUNK, cb), lambda g, j: (jnp.minimum(g + 1, nch - 1), j)),
            pl.BlockSpec((9, cb), lambda g, j: (0, j)),
            pl.BlockSpec((1, cb), lambda g, j: (0, j)),
        ],
        out_specs=pl.BlockSpec((M_CHUNK, cb), lambda g, j: (g, j)),
        out_shape=jax.ShapeDtypeStruct((ntok, c2), F32),
        scratch_shapes=[pltpu.VMEM((3 * M_CHUNK, cb), F32)],
        compiler_params=_cparams(("arbitrary", "arbitrary")),
        name="qk_conv",
    )(proj, proj, proj, conv_w.reshape(9, c2), conv_b.reshape(1, c2))


def _mlstm_direction(q_ref, k_ref, v_ref, gc_ref, gr_ref, gbc_ref, gbr_ref, h_ref,
                     ct_ref, n_ref, m_ref, *, rev):
    L = M_CHUNK
    t_i = lax.broadcasted_iota(jnp.int32, (L, L), 0)
    s_i = lax.broadcasted_iota(jnp.int32, (L, L), 1)
    keep = (s_i >= t_i) if rev else (s_i <= t_i)
    tri_ts = keep.astype(F32).astype(BF16)
    tri_st = ((t_i >= s_i) if rev else (t_i <= s_i)).astype(F32).astype(BF16)

    gcol = gc_ref[...] + gbc_ref[...]
    grow = gr_ref[...] + gbr_ref[...]
    bcol = _dot01_left(tri_ts, _log_sigmoid(gcol))
    brow = _dot01_right(_log_sigmoid(grow), tri_st)
    last = 0 if rev else L - 1
    d0 = 2 * M_HEADS if rev else 0
    scale = M_HEAD_DIM ** -0.5

    for h in range(M_HEADS):
        il, fl = d0 + h, d0 + M_HEADS + h
        sl = slice(h * M_HEAD_DIM, (h + 1) * M_HEAD_DIM)
        st = (M_HEADS if rev else 0) + h
        bc = jnp.broadcast_to(bcol[:, fl:fl + 1], (L, L))
        ic = jnp.broadcast_to(gcol[:, il:il + 1], (L, L))
        br = brow[fl:fl + 1, :]
        ir = grow[il:il + 1, :]
        m_prev = m_ref[st:st + 1, :]
        n_prev = n_ref[st:st + 1, :]
        ct_prev = ct_ref[st]

        d = jnp.where(keep, bc - br + ir, -jnp.inf)
        inter = bc + m_prev
        m_t = jnp.maximum(inter, jnp.max(d, axis=1, keepdims=True))
        a_t = jnp.exp(inter - m_t)
        q = (q_ref[:, sl] * scale).astype(BF16)
        k = k_ref[:, sl]
        v = v_ref[:, sl]
        k_t = k.T.astype(BF16)
        sc = jnp.dot(q, k_t, preferred_element_type=F32) * jnp.exp(d - m_t)
        num = a_t * jnp.dot(q, ct_prev.astype(BF16), preferred_element_type=F32) + _bdot(sc, v)
        qn = jnp.sum(q_ref[:, sl] * scale * n_prev, axis=1, keepdims=True)
        den = a_t * qn + jnp.sum(sc, axis=1, keepdims=True)
        h_ref[:, sl] = num / jnp.maximum(jnp.abs(den), jnp.exp(-m_t))

        g = bc[last:last + 1, :]
        dl = g - bc + ic
        m_new = jnp.maximum(g + m_prev, jnp.max(dl, axis=0, keepdims=True))
        wl = jnp.exp(dl - m_new)
        a_l = jnp.exp(g + m_prev - m_new)
        ct_ref[st] = a_l * ct_prev + jnp.dot(k_t, (v * wl).astype(BF16), preferred_element_type=F32)
        n_ref[st:st + 1, :] = a_l * n_prev + jnp.sum(wl * k, axis=0, keepdims=True)
        m_ref[st:st + 1, :] = m_new


def _mlstm_kernel(qf, kf, vf, gcf, grf, qb, kb, vb, gcb, grb, gbc, gbr, hf, hb, ct_ref, n_ref, m_ref):
    @pl.when(pl.program_id(1) == 0)
    def _zero_state():
        ct_ref[...] = jnp.zeros(ct_ref.shape, F32)
        n_ref[...] = jnp.zeros(n_ref.shape, F32)
        m_ref[...] = jnp.full(m_ref.shape, M_STAB_INIT, F32)

    _mlstm_direction(qf, kf, vf, gcf, grf, gbc, gbr, hf, ct_ref, n_ref, m_ref, rev=False)
    _mlstm_direction(qb, kb, vb, gcb, grb, gbc, gbr, hb, ct_ref, n_ref, m_ref, rev=True)


def _mlstm(qkc, proj, gates, gates_t, gate_b, *, batch, ctx_len, seq):
    ntok = qkc.shape[0]
    width = M_HEADS * M_HEAD_DIM
    ccb = ctx_len // M_CHUNK
    xcb = seq // M_CHUNK
    ns = ccb + xcb
    lat0 = batch * ccb

    def fwd(b, s):
        return jnp.where(s < ccb, b * ccb + s, lat0 + b * xcb + (s - ccb))

    def bwd(b, s):
        return jnp.where(s < ccb, b * ccb + (ccb - 1 - s), lat0 + b * xcb + (ns - 1 - s))

    def specs(ch):
        return [
            pl.BlockSpec((M_CHUNK, width), lambda b, s: (ch(b, s), 0)),
            pl.BlockSpec((M_CHUNK, width), lambda b, s: (ch(b, s), 1)),
            pl.BlockSpec((M_CHUNK, width), lambda b, s: (ch(b, s), 2)),
            pl.BlockSpec((M_CHUNK, LANES), lambda b, s: (ch(b, s), 0)),
            pl.BlockSpec((N_GATES, M_CHUNK), lambda b, s: (0, ch(b, s))),
        ]

    gb_lane = jnp.zeros((1, LANES), F32).at[0, :N_GATES].set(gate_b)
    gb_sub = gate_b.reshape(N_GATES, 1)
    out = jax.ShapeDtypeStruct((ntok, width), F32)
    return pl.pallas_call(
        _mlstm_kernel,
        grid=(batch, ns),
        in_specs=specs(fwd) + specs(bwd) + [
            pl.BlockSpec((1, LANES), lambda b, s: (0, 0)),
            pl.BlockSpec((N_GATES, 1), lambda b, s: (0, 0)),
        ],
        out_specs=[
            pl.BlockSpec((M_CHUNK, width), lambda b, s: (fwd(b, s), 0)),
            pl.BlockSpec((M_CHUNK, width), lambda b, s: (bwd(b, s), 0)),
        ],
        out_shape=[out, out],
        scratch_shapes=[
            pltpu.VMEM((2 * M_HEADS, M_HEAD_DIM, M_HEAD_DIM), F32),
            pltpu.VMEM((2 * M_HEADS, M_HEAD_DIM), F32),
            pltpu.VMEM((2 * M_HEADS, LANES), F32),
        ],
        compiler_params=_cparams(("arbitrary", "arbitrary")),
        name="mlstm_scan",
    )(qkc, qkc, proj, gates, gates_t, qkc, qkc, proj, gates, gates_t, gb_lane, gb_sub)


def _mix_kernel(hf_ref, hb_ref, o_ref, u_ref, vg_ref, ga_ref, gb_ref, z_ref, mod_ref,
                mnw_ref, snw_ref, sw_ref, sbt_ref, bb_ref, pa_ref, pb_ref, wo_ref, lnw_ref, lnb_ref,
                out_ref, hn_ref, mixed_ref, *, alpha):
    tm, width = z_ref.shape
    hs = hf_ref[...] + hb_ref[...]
    for h in range(M_HEADS):
        sl = slice(h * M_HEAD_DIM, (h + 1) * M_HEAD_DIM)
        hn_ref[:, sl] = _standardize(hs[:, sl])
    h_a = jax.nn.sigmoid(o_ref[...]) * (hn_ref[...] * mnw_ref[...])

    zv = (_standardize(jax.nn.gelu(vg_ref[...])) * snw_ref[...]).astype(BF16)
    gdim = width // S_GROUPS
    for ci in range(tm // S_CHUNK):
        rs = slice(ci * S_CHUNK, (ci + 1) * S_CHUNK)
        for g in range(S_GROUPS):
            cs = slice(g * gdim, (g + 1) * gdim)
            mixed_ref[rs, cs] = (jnp.dot(sw_ref[g], zv[rs, cs], preferred_element_type=F32)
                                 + sbt_ref[:, g:g + 1])
    h_b = jax.nn.gelu(u_ref[...]) * mixed_ref[...]

    ba = bb_ref[:, :width]
    bb = bb_ref[:, width:]
    merged = (jax.nn.sigmoid(ga_ref[...] + ba) * jnp.dot(h_a.astype(BF16), pa_ref[...], preferred_element_type=F32)
              + jax.nn.sigmoid(gb_ref[...] + bb) * jnp.dot(h_b.astype(BF16), pb_ref[...], preferred_element_type=F32))
    y = jnp.dot(merged.astype(BF16), wo_ref[...], preferred_element_type=F32)
    gate1 = mod_ref[0][2:3, :]
    out_ref[...] = _standardize(alpha * z_ref[...] + gate1 * y) * lnw_ref[...] + lnb_ref[...]


def _mix(hf, hb, proj, z, mod, mnw, snw, sgu_w_bf, sgu_bt, branch_b, pa_bf, pb_bf, wo_bf, lnw, lnb,
         *, alpha, nctx, seq):
    ntok, d = z.shape
    tm = _row_tile(nctx, seq, 256)
    row = lambda j: pl.BlockSpec((tm, d), lambda i: (i, j))
    const = lambda shape: pl.BlockSpec(shape, lambda i: (0,) * len(shape))
    return pl.pallas_call(
        functools.partial(_mix_kernel, alpha=alpha),
        grid=(ntok // tm,),
        in_specs=[
            row(0), row(0),
            row(3), row(4), row(5), row(6), row(7),
            row(0),
            pl.BlockSpec((1, 6, d), lambda i: (_seg_of_row(i * tm, nctx, seq), 0, 0)),
            const((1, d)), const((1, d)),
            const((S_GROUPS, S_CHUNK, S_CHUNK)), const((S_CHUNK, LANES)),
            const((1, 2 * d)),
            const((d, d)), const((d, d)), const((d, d)),
            const((1, d)), const((1, d)),
        ],
        out_specs=pl.BlockSpec((tm, d), lambda i: (i, 0)),
        out_shape=jax.ShapeDtypeStruct((ntok, d), F32),
        scratch_shapes=[pltpu.VMEM((tm, d), F32), pltpu.VMEM((tm, d), F32)],
        compiler_params=_cparams(("arbitrary",)),
        name="mixer_merge",
    )(hf, hb, proj, proj, proj, proj, proj, z, mod, mnw, snw, sgu_w_bf, sgu_bt, branch_b,
      pa_bf, pb_bf, wo_bf, lnw, lnb)


def _extract_top(s, rounds):
    rowi = lax.broadcasted_iota(jnp.int32, s.shape, 0).astype(F32)
    rank = jnp.full(s.shape, _NOT_RANKED, F32)
    vals = []
    for r in range(rounds):
        m = jnp.max(s, axis=0, keepdims=True)
        first = jnp.min(jnp.where(s == m, rowi, float(s.shape[0])), axis=0, keepdims=True)
        sel = rowi == first
        rank = jnp.where(sel, float(r), rank)
        s = jnp.where(sel, -jnp.inf, s)
        vals.append(m)
    return vals, rank


def _route_kernel(z_ref, mod_ref, wq_ref, keys_ref, hx_ref, r1_ref, e1_ref, n0_ref, e0_ref,
                  q_scr, a_scr, b_scr, cand_scr, sel_scr, n_scr):
    h = pl.program_id(1)

    @pl.when(h == 0)
    def _project():
        m = mod_ref[0]
        hx = z_ref[...] * (1.0 + m[4:5, :]) + m[3:4, :]
        hx_bf = hx.astype(BF16)
        hx_ref[...] = hx_bf
        q_scr[...] = jnp.dot(hx_bf, wq_ref[...], preferred_element_type=F32)

    qh = q_scr[:, pl.ds(pl.multiple_of(h * 2 * PEER_HALF, 2 * PEER_HALF), 2 * PEER_HALF)]
    nt = (((1,), (1,)), ((), ()))
    s0 = lax.dot_general(keys_ref[0, 0], qh[:, :PEER_HALF].astype(BF16), nt, preferred_element_type=F32)
    s1 = lax.dot_general(keys_ref[0, 1], qh[:, PEER_HALF:].astype(BF16), nt, preferred_element_type=F32)

    a_vals, rank0 = _extract_top(s0, PEER_TOPK)
    b_vals, rank1 = _extract_top(s1, PEER_TOPK)
    for r in range(PEER_TOPK):
        a_scr[r:r + 1, :] = a_vals[r]
        b_scr[r:r + 1, :] = b_vals[r]

    cand_scr[...] = jnp.full(cand_scr.shape, -jnp.inf, F32)
    for r in range(PEER_TOPK):
        off, cnt = _CAND_OFFS[r], _CAND_COUNTS[r]
        cand_scr[off:off + cnt, :] = a_vals[r] + b_scr[0:cnt, :]
    cand = cand_scr[...]
    _, rank2 = _extract_top(cand, PEER_TOPK)
    chosen = rank2 < float(PEER_TOPK)
    sel_scr[...] = jnp.where(chosen, 1.0, 0.0)
    zsum = jnp.sum(jnp.where(chosen, jnp.exp(cand - cand[0:1, :]), 0.0), axis=0, keepdims=True)

    n0 = jnp.zeros(rank0.shape, F32)
    for r in range(PEER_TOPK):
        off, cnt = _CAND_OFFS[r], _CAND_COUNTS[r]
        n_r = jnp.sum(sel_scr[off:off + cnt, :], axis=0, keepdims=True)
        n0 = jnp.where(rank0 == float(r), n_r, n0)

    r1_ref[0] = rank1
    e1_ref[0] = jnp.exp(s1 - b_vals[0])
    n0_ref[0] = n0
    e0_ref[0] = jnp.exp(s0 - a_vals[0]) / zsum


def _route(z, mod, wq_bf, keys_bf, *, nctx, seq):
    ntok, d = z.shape
    te = _row_tile(nctx, seq, 256)
    nq = wq_bf.shape[1]
    plane = jax.ShapeDtypeStruct((PEER_HEADS, N_KEYS, ntok), F32)
    plane_spec = pl.BlockSpec((1, N_KEYS, te), lambda i, h: (h, 0, i))
    return pl.pallas_call(
        _route_kernel,
        grid=(ntok // te, PEER_HEADS),
        in_specs=[
            pl.BlockSpec((te, d), lambda i, h: (i, 0)),
            pl.BlockSpec((1, 6, d), lambda i, h: (_seg_of_row(i * te, nctx, seq), 0, 0)),
            pl.BlockSpec((d, nq), lambda i, h: (0, 0)),
            pl.BlockSpec((1, 2, N_KEYS, PEER_HALF), lambda i, h: (h, 0, 0, 0)),
        ],
        out_specs=[pl.BlockSpec((te, d), lambda i, h: (i, 0)), plane_spec, plane_spec, plane_spec, plane_spec],
        out_shape=[jax.ShapeDtypeStruct((ntok, d), BF16), plane, plane, plane, plane],
        scratch_shapes=[
            pltpu.VMEM((te, nq), F32),
            pltpu.VMEM((PEER_TOPK, te), F32),
            pltpu.VMEM((PEER_TOPK, te), F32),
            pltpu.VMEM((_N_CAND_PAD, te), F32),
            pltpu.VMEM((_N_CAND_PAD, te), F32),
            pltpu.VMEM((PEER_TOPK, te), F32),
        ],
        compiler_params=_cparams(("arbitrary", "arbitrary")),
        name="peer_route",
    )(z, mod, wq_bf, keys_bf)


def _experts_kernel(hx_ref, u_ref, vt_ref, r1_ref, e1_ref, n0_ref, e0_ref, z_ref, mod_ref, lnw_ref, lnb_ref,
                    out_ref, acc_ref, a_ref, *, alpha):
    e = pl.program_id(1)

    @pl.when(e == 0)
    def _zero():
        acc_ref[...] = jnp.zeros(acc_ref.shape, F32)

    nt = (((1,), (1,)), ((), ()))
    st = lax.dot_general(u_ref[...], hx_ref[...], nt, preferred_element_type=F32)
    for ii in range(u_ref.shape[0] // N_KEYS):
        rs = slice(ii * N_KEYS, (ii + 1) * N_KEYS)
        gate = jnp.zeros((N_KEYS, hx_ref.shape[0]), F32)
        for h in range(PEER_HEADS):
            keep = r1_ref[h] < n0_ref[h, ii:ii + 1, :]
            gate = gate + jnp.where(keep, e1_ref[h] * e0_ref[h, ii:ii + 1, :], 0.0)
        a_ref[rs, :] = (gate * jax.nn.gelu(st[rs, :])).astype(BF16)
    acc_ref[...] += jnp.dot(vt_ref[...], a_ref[...], preferred_element_type=F32)

    @pl.when(e == pl.num_programs(1) - 1)
    def _finish():
        gate2 = mod_ref[0][5:6, :]
        y = acc_ref[...].T
        out_ref[...] = _standardize(alpha * z_ref[...] + gate2 * y) * lnw_ref[...] + lnb_ref[...]


def _experts(hx_bf, u_bf, vt_bf, r1, e1, n0, e0, z, mod, lnw, lnb, *, alpha, nctx, seq):
    ntok, d = z.shape
    ne = u_bf.shape[0]
    tf = _row_tile(nctx, seq, 512)
    eb = 8 * N_KEYS
    ib = eb // N_KEYS
    return pl.pallas_call(
        functools.partial(_experts_kernel, alpha=alpha),
        grid=(ntok // tf, ne // eb),
        in_specs=[
            pl.BlockSpec((tf, d), lambda i, e: (i, 0)),
            pl.BlockSpec((eb, d), lambda i, e: (e, 0)),
            pl.BlockSpec((d, eb), lambda i, e: (0, e)),
            pl.BlockSpec((PEER_HEADS, N_KEYS, tf), lambda i, e: (0, 0, i)),
            pl.BlockSpec((PEER_HEADS, N_KEYS, tf), lambda i, e: (0, 0, i)),
            pl.BlockSpec((PEER_HEADS, ib, tf), lambda i, e: (0, e, i)),
            pl.BlockSpec((PEER_HEADS, ib, tf), lambda i, e: (0, e, i)),
            pl.BlockSpec((tf, d), lambda i, e: (i, 0)),
            pl.BlockSpec((1, 6, d), lambda i, e: (_seg_of_row(i * tf, nctx, seq), 0, 0)),
            pl.BlockSpec((1, d), lambda i, e: (0, 0)),
            pl.BlockSpec((1, d), lambda i, e: (0, 0)),
        ],
        out_specs=pl.BlockSpec((tf, d), lambda i, e: (i, 0)),
        out_shape=jax.ShapeDtypeStruct((ntok, d), F32),
        scratch_shapes=[pltpu.VMEM((d, tf), F32), pltpu.VMEM((eb, tf), BF16)],
        compiler_params=_cparams(("arbitrary", "arbitrary")),
        name="peer_experts",
    )(hx_bf, u_bf, vt_bf, r1, e1, n0, e0, z, mod, lnw, lnb)


def kernel(x, c, ctx, c_ctx, w_mod, b_mod, w_in, qk_conv_w, qk_conv_b, gate_b, mlstm_norm_w, sgu_norm_w,
           sgu_w, sgu_b, branch_b, p_a, p_b, w_out, ln_w, ln_b, peer_wq, peer_keys, expert_u, expert_v):
    batch, seq, d = x.shape
    ctx_len = ctx.shape[1]
    depth = w_mod.shape[0]
    width = M_HEADS * M_HEAD_DIM
    assert d == width and seq % (2 * M_CHUNK) == 0 and ctx_len % M_CHUNK == 0 and batch + 1 <= 8
    nctx = batch * ctx_len
    alpha = float((2 * depth) ** 0.25)

    z = jnp.concatenate([ctx.reshape(nctx, d), x.reshape(batch * seq, d)], axis=0)

    cvec = jnp.zeros((8, d), F32).at[0].set(c_ctx).at[1:1 + batch].set(c)
    mod_all = _mod_table(cvec, w_mod, b_mod).reshape(depth, 8, 6, d)

    gate_lo, gate_hi = 4 * width, 4 * width + N_GATES
    tm = _row_tile(nctx, seq, 512)
    for l in range(depth):
        mod = mod_all[l]
        w_main = jnp.concatenate([w_in[l][:, :gate_lo], w_in[l][:, gate_hi:]], axis=1).astype(BF16)
        w_gate = jnp.zeros((d, LANES), F32).at[:, :N_GATES].set(w_in[l][:, gate_lo:gate_hi]).astype(BF16)

        proj = _mod_matmul(z, mod, w_main, which=0, nctx=nctx, seq=seq, tm=tm, tn=2048, name="in_proj")
        gates = _mod_matmul(z, mod, w_gate, which=0, nctx=nctx, seq=seq, tm=tm, tn=LANES, name="gate_proj")
        gates_t = gates[:, :N_GATES].T

        qkc = _qk_conv(proj, qk_conv_w[l], qk_conv_b[l], batch=batch, ctx_len=ctx_len, seq=seq)
        hf, hb = _mlstm(qkc, proj, gates, gates_t, gate_b[l], batch=batch, ctx_len=ctx_len, seq=seq)

        sgu_bt = jnp.zeros((S_CHUNK, LANES), F32).at[:, :S_GROUPS].set(sgu_b[l].T)
        z = _mix(hf, hb, proj, z, mod, mlstm_norm_w[l].reshape(1, d), sgu_norm_w[l].reshape(1, d),
                 sgu_w[l].astype(BF16), sgu_bt, branch_b[l].reshape(1, 2 * d),
                 p_a[l].astype(BF16), p_b[l].astype(BF16), w_out[l].astype(BF16),
                 ln_w[l, 0].reshape(1, d), ln_b[l, 0].reshape(1, d), alpha=alpha, nctx=nctx, seq=seq)

        hx_bf, r1, e1, n0, e0 = _route(z, mod, peer_wq[l].astype(BF16), peer_keys[l].astype(BF16),
                                       nctx=nctx, seq=seq)
        z = _experts(hx_bf, expert_u[l].astype(BF16), expert_v[l].T.astype(BF16), r1, e1, n0, e0, z, mod,
                     ln_w[l, 1].reshape(1, d), ln_b[l, 1].reshape(1, d), alpha=alpha, nctx=nctx, seq=seq)

    return z[nctx:].reshape(batch, seq, d)
```
